```python
import jax, jax.numpy as jnp
from jax import lax
import numpy as np

D_MODEL = 1024
BATCH = 2
SEQ = 16384
DEPTH = 1

GRID_W = 64
CTX_LEN = 256
HEAD_DIM = 128
N_Q_HEADS = 4
N_KV_HEADS = 2
Q_PER_KV = N_Q_HEADS // N_KV_HEADS
ATTN_WIDTH = N_Q_HEADS * HEAD_DIM
KV_WIDTH = N_KV_HEADS * HEAD_DIM
AXIS_DIM = HEAD_DIM // 2
ROPE_THETA = 10000.0
Q_BLOCK = 128
CONV_WIDTH = D_MODEL - ATTN_WIDTH
CONV_K = 3
MIX_WIDTH = ATTN_WIDTH + CONV_WIDTH
Q_END = ATTN_WIDTH
K_END = Q_END + KV_WIDTH
KV_END = K_END + KV_WIDTH
CB_END = KV_END + CONV_WIDTH
CC_END = CB_END + CONV_WIDTH
IN_COLS = CC_END + CONV_WIDTH
N_EXPERTS = 32
TOP_K = 4
D_FF = D_MODEL
SWIGLU_LIMIT = 7.0
SWIGLU_ALPHA = 1.702
MOE_BLOCK = 128
N_MOD = 6
NORM_EPS = 1e-6

kernel_name = "hybrid_conv_gqa_moe_prefix_dit_block"


def rmsnorm(x, gain):
    x32 = x.astype(jnp.float32)
    y = x32 * lax.rsqrt(jnp.mean(x32 * x32, axis=-1, keepdims=True) + NORM_EPS)
    return y.astype(x.dtype) * gain


def ada_params(cond, w_ada, b_ada):
    mod = jax.nn.silu(cond) @ w_ada + b_ada
    return jnp.split(mod[..., None, :], N_MOD, axis=-1)


def modulate(x, gain, shift, scale):
    return rmsnorm(x, gain) * (1.0 + scale) + shift


def split_in(p):
    return jnp.split(p, [Q_END, K_END, KV_END, CB_END, CC_END], axis=-1)


def to_heads(t, n_heads, gain=None):
    b, n, _ = t.shape
    t = t.reshape(b, n, n_heads, HEAD_DIM)
    return t if gain is None else rmsnorm(t, gain)


def axial_rope_tables(n_tokens):
    rows = n_tokens // GRID_W
    row = jnp.broadcast_to(jnp.arange(rows, dtype=jnp.float32)[:, None], (rows, GRID_W)).reshape(-1)
    col = jnp.broadcast_to(jnp.arange(GRID_W, dtype=jnp.float32)[None, :], (rows, GRID_W)).reshape(-1)
    inv_freq = ROPE_THETA ** (-jnp.arange(0, AXIS_DIM, 2, dtype=jnp.float32) / AXIS_DIM)
    ang = jnp.stack([row[:, None] * inv_freq, col[:, None] * inv_freq], axis=1)
    return jnp.cos(ang), jnp.sin(ang)


def apply_axial_rope(t, cos, sin):
    b, n, h, _ = t.shape
    tr = t.astype(jnp.float32).reshape(b, n, h, 2, 2, AXIS_DIM // 2)
    t1, t2 = tr[..., 0, :], tr[..., 1, :]
    cs, sn = cos[None, :, None], sin[None, :, None]
    out = jnp.stack([t1 * cs - t2 * sn, t2 * cs + t1 * sn], axis=-2)
    return out.reshape(b, n, h, HEAD_DIM).astype(t.dtype)


def dense_attention(q, k, v):
    b, nq, _, _ = q.shape
    qg = q.reshape(b, nq, N_KV_HEADS, Q_PER_KV, HEAD_DIM)
    s = jnp.einsum('bqkgd,bskd->bkgqs', qg, k, preferred_element_type=jnp.float32) * (HEAD_DIM ** -0.5)
    p = jax.nn.softmax(s, axis=-1).astype(v.dtype)
    o = jnp.einsum('bkgqs,bskd->bqkgd', p, v)
    return o.reshape(b, nq, ATTN_WIDTH)


def blocked_attention(q, k, v):
    b, n, _, _ = q.shape
    nblk = n // Q_BLOCK
    qb = q.reshape(b, nblk, Q_BLOCK, N_Q_HEADS, HEAD_DIM).transpose(1, 0, 2, 3, 4)
    o = lax.map(lambda qblk: dense_attention(qblk, k, v), qb)
    return o.transpose(1, 0, 2, 3).reshape(b, n, ATTN_WIDTH)


def short_conv_mixer(cb, cc, cx, conv_w):
    u = cc * cx
    up = jnp.pad(u, ((0, 0), (1, 1), (0, 0)))
    y = up[:, :-2] * conv_w[0] + up[:, 1:-1] * conv_w[1] + up[:, 2:] * conv_w[2]
    return cb * y


def clamped_swiglu(gu):
    glu, lin = jnp.split(gu, 2, axis=-1)
    glu = jnp.minimum(glu, SWIGLU_LIMIT)
    lin = jnp.clip(lin, -SWIGLU_LIMIT, SWIGLU_LIMIT)
    return glu * jax.nn.sigmoid(SWIGLU_ALPHA * glu) * (lin + 1.0)


def moe_ffn(h, w_router, b_router, w_gate_up, b_gate_up, w_down, b_down):
    n_tok, d = h.shape
    logits = jnp.dot(h, w_router, preferred_element_type=jnp.float32) + b_router.astype(jnp.float32)
    top_logit, top_idx = lax.top_k(logits, TOP_K)
    gate = jax.nn.softmax(top_logit, axis=-1)
    n_pair = n_tok * TOP_K
    n_blocks = -(-(n_pair + N_EXPERTS * (MOE_BLOCK - 1)) // MOE_BLOCK)
    n_slots = n_blocks * MOE_BLOCK
    flat_e = top_idx.reshape(-1).astype(jnp.int32)
    flat_tok = jnp.repeat(jnp.arange(n_tok, dtype=jnp.int32), TOP_K, total_repeat_length=n_pair)
    flat_gate = gate.reshape(-1)
    order = jnp.argsort(flat_e, stable=True)
    e_sorted = flat_e[order]
    counts = jnp.zeros((N_EXPERTS,), jnp.int32).at[flat_e].add(1)
    padded = (counts + MOE_BLOCK - 1) // MOE_BLOCK * MOE_BLOCK
    start = jnp.cumsum(counts) - counts
    pad_end = jnp.cumsum(padded)
    pad_start = pad_end - padded
    slot = pad_start[e_sorted] + jnp.arange(n_pair, dtype=jnp.int32) - start[e_sorted]
    slot_tok = jnp.full((n_slots,), n_tok, jnp.int32).at[slot].set(flat_tok[order])
    slot_gate = jnp.zeros((n_slots,), jnp.float32).at[slot].set(flat_gate[order])
    block_start = jnp.arange(n_blocks, dtype=jnp.int32) * MOE_BLOCK
    block_expert = jnp.minimum(jnp.searchsorted(pad_end, block_start, side='right'), N_EXPERTS - 1)
    h_pad = jnp.concatenate([h, jnp.zeros((1, d), h.dtype)], axis=0)

    def expert_block(args):
        tok, g, e = args
        xb = h_pad[tok]
        gu = xb @ w_gate_up[e] + b_gate_up[e]
        y = clamped_swiglu(gu) @ w_down[e] + b_down[e]
        return y * g[:, None].astype(y.dtype)

    y = lax.map(expert_block, (slot_tok.reshape(n_blocks, MOE_BLOCK),
                               slot_gate.reshape(n_blocks, MOE_BLOCK), block_expert))
    out = jnp.zeros((n_tok + 1, d), y.dtype).at[slot_tok].add(y.reshape(n_slots, d))
    return out[:n_tok]


def setup_inputs(seed: int = 0) -> dict:
    key = jax.random.key(seed)
    ks = jax.random.split(key, 20)
    f32 = jnp.float32
    nrm = lambda k, shape, s: jax.random.normal(k, shape, f32) * s
    return {
        "x": nrm(ks[0], (BATCH, SEQ, D_MODEL), 1.0),
        "c": nrm(ks[1], (BATCH, D_MODEL), 1.0),
        "ctx": nrm(ks[2], (BATCH, CTX_LEN, D_MODEL), 1.0),
        "c_ctx": nrm(ks[3], (D_MODEL,), 1.0),
        "w_ada": nrm(ks[4], (DEPTH, D_MODEL, N_MOD * D_MODEL), 0.2 * D_MODEL ** -0.5),
        "b_ada": nrm(ks[5], (DEPTH, N_MOD * D_MODEL), 0.01),
        "g_norm1": 1.0 + nrm(ks[6], (DEPTH, D_MODEL), 0.01),
        "g_norm2": 1.0 + nrm(ks[7], (DEPTH, D_MODEL), 0.01),
        "w_in": nrm(ks[8], (DEPTH, D_MODEL, IN_COLS), D_MODEL ** -0.5),
        "g_q": 1.0 + nrm(ks[9], (DEPTH, HEAD_DIM), 0.01),
        "g_k": 1.0 + nrm(ks[10], (DEPTH, HEAD_DIM), 0.01),
        "w_conv": nrm(ks[11], (DEPTH, CONV_K, CONV_WIDTH), CONV_K ** -0.5),
        "w_out": nrm(ks[12], (DEPTH, MIX_WIDTH, D_MODEL), MIX_WIDTH ** -0.5),
        "w_router": nrm(ks[13], (DEPTH, D_MODEL, N_EXPERTS), D_MODEL ** -0.5),
        "b_router": nrm(ks[14], (DEPTH, N_EXPERTS), 0.01),
        "w_gate_up": nrm(ks[15], (DEPTH, N_EXPERTS, D_MODEL, 2 * D_FF), D_MODEL ** -0.5),
        "b_gate_up": nrm(ks[16], (DEPTH, N_EXPERTS, 2 * D_FF), 0.01),
        "w_down": nrm(ks[17], (DEPTH, N_EXPERTS, D_FF, D_MODEL), D_FF ** -0.5),
        "b_down": nrm(ks[18], (DEPTH, N_EXPERTS, D_MODEL), 0.01),
        "g_final": 1.0 + nrm(ks[19], (D_MODEL,), 0.01),
    }


def reference(x, c, ctx, c_ctx, w_ada, b_ada, g_norm1, g_norm2, w_in, g_q, g_k, w_conv, w_out,
              w_router, b_router, w_gate_up, b_gate_up, w_down, b_down, g_final):
    b, n, d = x.shape
    cos, sin = axial_rope_tables(n)
    for layer in range(DEPTH):
        last = layer == DEPTH - 1
        sh1, sc1, gt1, sh2, sc2, gt2 = ada_params(c, w_ada[layer], b_ada[layer])
        csh1, csc1, cgt1, csh2, csc2, cgt2 = ada_params(c_ctx, w_ada[layer], b_ada[layer])
        w_in_l = w_in[layer]

        hc = modulate(ctx, g_norm1[layer], csh1, csc1)
        if last:
            kc, vc = jnp.split(hc @ w_in_l[:, Q_END:KV_END], 2, axis=-1)
        else:
            qc, kc, vc, cbc, ccc, cxc = split_in(hc @ w_in_l)
        kc = to_heads(kc, N_KV_HEADS, g_k[layer])
        vc = to_heads(vc, N_KV_HEADS)
        if not last:
            qc = to_heads(qc, N_Q_HEADS, g_q[layer])
            mix_c = jnp.concatenate([dense_attention(qc, kc, vc),
                                     short_conv_mixer(cbc, ccc, cxc, w_conv[layer])], axis=-1)
            ctx_next = ctx + cgt1 * (mix_c @ w_out[layer])
            hc2 = modulate(ctx_next, g_norm2[layer], csh2, csc2).reshape(-1, d)
            ctx_next = ctx_next + cgt2 * moe_ffn(hc2, w_router[layer], b_router[layer], w_gate_up[layer],
                                                 b_gate_up[layer], w_down[layer], b_down[layer]).reshape(ctx.shape)

        h = modulate(x, g_norm1[layer], sh1, sc1)
        q, k, v, cb, cc, cx = split_in(h @ w_in_l)
        q = apply_axial_rope(to_heads(q, N_Q_HEADS, g_q[layer]), cos, sin)
        k = apply_axial_rope(to_heads(k, N_KV_HEADS, g_k[layer]), cos, sin)
        v = to_heads(v, N_KV_HEADS)
        k_all = jnp.concatenate([k, kc], axis=1)
        v_all = jnp.concatenate([v, vc], axis=1)
        attn = blocked_attention(q, k_all, v_all)
        conv = short_conv_mixer(cb, cc, cx, w_conv[layer])
        x = x + gt1 * (jnp.concatenate([attn, conv], axis=-1) @ w_out[layer])

        h2 = modulate(x, g_norm2[layer], sh2, sc2).reshape(-1, d)
        x = x + gt2 * moe_ffn(h2, w_router[layer], b_router[layer], w_gate_up[layer], b_gate_up[layer],
                              w_down[layer], b_down[layer]).reshape(b, n, d)
        if not last:
            ctx = ctx_next
    return rmsnorm(x, g_final)
```

```python
import functools

import jax
import jax.numpy as jnp
import numpy as np
from jax import lax
from jax.experimental import pallas as pl
from jax.experimental.pallas import tpu as pltpu

F32 = jnp.float32
BF16 = jnp.bfloat16

D_MODEL = 1024
HEAD_DIM = 128
N_Q_HEADS = 4
N_KV_HEADS = 2
Q_PER_KV = N_Q_HEADS // N_KV_HEADS
ATTN_WIDTH = N_Q_HEADS * HEAD_DIM
KV_WIDTH = N_KV_HEADS * HEAD_DIM
AXIS_DIM = HEAD_DIM // 2
ROPE_HALF = AXIS_DIM // 2
ROPE_THETA = 10000.0
GRID_W = 64
CONV_WIDTH = D_MODEL - ATTN_WIDTH
CONV_K = 3
Q_END = ATTN_WIDTH
K_END = Q_END + KV_WIDTH
KV_END = K_END + KV_WIDTH
CB_END = KV_END + CONV_WIDTH
CC_END = CB_END + CONV_WIDTH
IN_COLS = CC_END + CONV_WIDTH
N_EXPERTS = 32
TOP_K = 4
D_FF = D_MODEL
SWIGLU_LIMIT = 7.0
SWIGLU_ALPHA = 1.702
N_MOD = 6
NORM_EPS = 1e-6
LOG2E = 1.4426950408889634

LANES = 128
BF16_SUBLANES = 16
VMEM_LIMIT = 56 * 1024 * 1024

ADA_ROWS = 8
ADA_TN = 1536
TM_IN = 512
TQ = 256
TK = 512
TM_OUT = 512
ROUTER_PAD = LANES
SLOT_CHUNK = 512
MOE_BLK = 256
TM_FIN = 512


def _cparams(sem, vmem=VMEM_LIMIT):
    return pltpu.CompilerParams(dimension_semantics=sem, vmem_limit_bytes=vmem)


def _ada_kernel(c_ref, w_ref, b_ref, o_ref):
    c = c_ref[...]
    a = c * jax.nn.sigmoid(c)
    o_ref[...] = jnp.dot(a, w_ref[...], preferred_element_type=F32,
                         precision=lax.Precision.HIGHEST) + b_ref[...]


def _ada(cond, w_ada, b_ada):
    n = w_ada.shape[1]
    return pl.pallas_call(
        _ada_kernel,
        grid=(n // ADA_TN,),
        in_specs=[
            pl.BlockSpec((ADA_ROWS, D_MODEL), lambda j: (0, 0)),
            pl.BlockSpec((D_MODEL, ADA_TN), lambda j: (0, j)),
            pl.BlockSpec((1, ADA_TN), lambda j: (0, j)),
        ],
        out_specs=pl.BlockSpec((ADA_ROWS, ADA_TN), lambda j: (0, j)),
        out_shape=jax.ShapeDtypeStruct((ADA_ROWS, n), F32),
        compiler_params=_cparams(("arbitrary",)),
        name="ada",
    )(cond, w_ada, b_ada.reshape(1, n))


def _modulated_norm(x, gain, shift, scale):
    ms = jnp.mean(x * x, axis=-1, keepdims=True)
    return x * lax.rsqrt(ms + NORM_EPS) * gain * (1.0 + scale) + shift


def _head_norm(t, gain):
    ms = jnp.mean(t * t, axis=-1, keepdims=True)
    return t * lax.rsqrt(ms + NORM_EPS) * gain


def _rope(t, cos, sin_signed):
    lane = lax.broadcasted_iota(jnp.int32, t.shape, 1)
    first = (lane & (AXIS_DIM - 1)) < ROPE_HALF
    fwd = pltpu.roll(t, HEAD_DIM - ROPE_HALF, axis=1)
    bwd = pltpu.roll(t, ROPE_HALF, axis=1)
    return t * cos + jnp.where(first, fwd, bwd) * sin_signed


def _rope_tables(n_tokens):
    rows = n_tokens // GRID_W
    row = jnp.broadcast_to(jnp.arange(rows, dtype=F32)[:, None], (rows, GRID_W)).reshape(-1)
    col = jnp.broadcast_to(jnp.arange(GRID_W, dtype=F32)[None, :], (rows, GRID_W)).reshape(-1)
    inv_freq = ROPE_THETA ** (-jnp.arange(0, AXIS_DIM, 2, dtype=F32) / AXIS_DIM)
    ang = jnp.stack([row[:, None] * inv_freq, col[:, None] * inv_freq], axis=1)
    cos, sin = jnp.cos(ang), jnp.sin(ang)
    cos_t = jnp.stack([cos, cos], axis=2).reshape(n_tokens, HEAD_DIM)
    sin_t = jnp.stack([-sin, sin], axis=2).reshape(n_tokens, HEAD_DIM)
    return cos_t, sin_t


def _inproj_kernel(x_ref, sh_ref, sc_ref, g1_ref, w_ref, gq_ref, gk_ref, cos_ref, sin_ref,
                   q_ref, kt_ref, v_ref, u_ref, cb_ref):
    h = _modulated_norm(x_ref[0], g1_ref[...], sh_ref[0], sc_ref[0]).astype(BF16)
    cos = cos_ref[...]
    sin = sin_ref[...]
    q_scale = HEAD_DIM ** -0.5 * LOG2E

    q = jnp.dot(h, w_ref[:, 0:Q_END], preferred_element_type=F32)
    for hh in range(N_Q_HEADS):
        t = _head_norm(q[:, hh * HEAD_DIM:(hh + 1) * HEAD_DIM], gq_ref[...])
        q_ref[0, hh] = (_rope(t, cos, sin) * q_scale).astype(BF16)

    k = jnp.dot(h, w_ref[:, Q_END:K_END], preferred_element_type=F32)
    for g in range(N_KV_HEADS):
        t = _head_norm(k[:, g * HEAD_DIM:(g + 1) * HEAD_DIM], gk_ref[...])
        kt_ref[0, g] = _rope(t, cos, sin).T.astype(BF16)

    v = jnp.dot(h, w_ref[:, K_END:KV_END], preferred_element_type=F32)
    for g in range(N_KV_HEADS):
        v_ref[0, g] = v[:, g * HEAD_DIM:(g + 1) * HEAD_DIM].astype(BF16)

    cb_ref[0] = jnp.dot(h, w_ref[:, KV_END:CB_END], preferred_element_type=F32).astype(BF16)
    cc = jnp.dot(h, w_ref[:, CB_END:CC_END], preferred_element_type=F32)
    cx = jnp.dot(h, w_ref[:, CC_END:IN_COLS], preferred_element_type=F32)
    u_ref[0] = (cc * cx).astype(BF16)


def _inproj(x, sh1, sc1, g1, w_in, g_q, g_k, cos_t, sin_t):
    b, s, d = x.shape
    nt = s // TM_IN
    mod_spec = pl.BlockSpec((1, 1, d), lambda bi, i: (bi, 0, 0))
    vec_spec = lambda n: pl.BlockSpec((1, n), lambda bi, i: (0, 0))
    return pl.pallas_call(
        _inproj_kernel,
        grid=(b, nt),
        in_specs=[
            pl.BlockSpec((1, TM_IN, d), lambda bi, i: (bi, i, 0)),
            mod_spec, mod_spec, vec_spec(d),
            pl.BlockSpec((d, IN_COLS), lambda bi, i: (0, 0)),
            vec_spec(HEAD_DIM), vec_spec(HEAD_DIM),
            pl.BlockSpec((TM_IN, HEAD_DIM), lambda bi, i: (i, 0)),
            pl.BlockSpec((TM_IN, HEAD_DIM), lambda bi, i: (i, 0)),
        ],
        out_specs=[
            pl.BlockSpec((1, N_Q_HEADS, TM_IN, HEAD_DIM), lambda bi, i: (bi, 0, i, 0)),
            pl.BlockSpec((1, N_KV_HEADS, HEAD_DIM, TM_IN), lambda bi, i: (bi, 0, 0, i)),
            pl.BlockSpec((1, N_KV_HEADS, TM_IN, HEAD_DIM), lambda bi, i: (bi, 0, i, 0)),
            pl.BlockSpec((1, TM_IN, CONV_WIDTH), lambda bi, i: (bi, i, 0)),
            pl.BlockSpec((1, TM_IN, CONV_WIDTH), lambda bi, i: (bi, i, 0)),
        ],
        out_shape=[
            jax.ShapeDtypeStruct((b, N_Q_HEADS, s, HEAD_DIM), BF16),
            jax.ShapeDtypeStruct((b, N_KV_HEADS, HEAD_DIM, s), BF16),
            jax.ShapeDtypeStruct((b, N_KV_HEADS, s, HEAD_DIM), BF16),
            jax.ShapeDtypeStruct((b, s, CONV_WIDTH), BF16),
            jax.ShapeDtypeStruct((b, s, CONV_WIDTH), BF16),
        ],
        compiler_params=_cparams(("parallel", "arbitrary")),
        name="inproj",
    )(x, sh1, sc1, g1.reshape(1, d), w_in, g_q.reshape(1, HEAD_DIM), g_k.reshape(1, HEAD_DIM),
      cos_t, sin_t)


def _ctx_kv_kernel(x_ref, sh_ref, sc_ref, g1_ref, w_ref, gk_ref, kt_ref, v_ref):
    h = _modulated_norm(x_ref[0], g1_ref[...], sh_ref[...], sc_ref[...]).astype(BF16)
    k = jnp.dot(h, w_ref[:, Q_END:K_END], preferred_element_type=F32)
    for g in range(N_KV_HEADS):
        t = _head_norm(k[:, g * HEAD_DIM:(g + 1) * HEAD_DIM], gk_ref[...])
        kt_ref[0, g] = t.T.astype(BF16)
    v = jnp.dot(h, w_ref[:, K_END:KV_END], preferred_element_type=F32)
    for g in range(N_KV_HEADS):
        v_ref[0, g] = v[:, g * HEAD_DIM:(g + 1) * HEAD_DIM].astype(BF16)


def _ctx_kv(ctx, csh1, csc1, g1, w_in, g_k):
    b, n_ctx, d = ctx.shape
    vec_spec = lambda n: pl.BlockSpec((1, n), lambda bi: (0, 0))
    return pl.pallas_call(
        _ctx_kv_kernel,
        grid=(b,),
        in_specs=[
            pl.BlockSpec((1, n_ctx, d), lambda bi: (bi, 0, 0)),
            vec_spec(d), vec_spec(d), vec_spec(d),
            pl.BlockSpec((d, IN_COLS), lambda bi: (0, 0)),
            vec_spec(HEAD_DIM),
        ],
        out_specs=[
            pl.BlockSpec((1, N_KV_HEADS, HEAD_DIM, n_ctx), lambda bi: (bi, 0, 0, 0)),
            pl.BlockSpec((1, N_KV_HEADS, n_ctx, HEAD_DIM), lambda bi: (bi, 0, 0, 0)),
        ],
        out_shape=[
            jax.ShapeDtypeStruct((b, N_KV_HEADS, HEAD_DIM, n_ctx), BF16),
            jax.ShapeDtypeStruct((b, N_KV_HEADS, n_ctx, HEAD_DIM), BF16),
        ],
        compiler_params=_cparams(("arbitrary",)),
        name="ctx_kv",
    )(ctx, csh1, csc1, g1.reshape(1, d), w_in, g_k.reshape(1, HEAD_DIM))


def _attn_kernel(q_ref, kt_ref, v_ref, ktc_ref, vc_ref, o_ref, *, n_lat):
    rows = Q_PER_KV * TQ
    q2 = q_ref[0].reshape(rows, HEAD_DIM)

    def step(kc, vc, carry):
        m, l, acc = carry
        s = jnp.dot(q2, kc, preferred_element_type=F32)
        m_new = jnp.maximum(m, jnp.max(s, axis=-1, keepdims=True))
        alpha = jnp.exp2(m - m_new)
        p = jnp.exp2(s - m_new)
        l = alpha * l + jnp.sum(p, axis=-1, keepdims=True)
        acc = alpha * acc + jnp.dot(p.astype(BF16), vc, preferred_element_type=F32)
        return m_new, l, acc

    def body(j, carry):
        off = pl.multiple_of(j * TK, TK)
        return step(kt_ref[0, 0, :, pl.ds(off, TK)], v_ref[0, 0, pl.ds(off, TK), :], carry)

    init = (jnp.full((rows, 1), -jnp.inf, F32), jnp.zeros((rows, 1), F32),
            jnp.zeros((rows, HEAD_DIM), F32))
    carry = lax.fori_loop(0, n_lat // TK, body, init)
    _, l, acc = step(ktc_ref[0, 0], vc_ref[0, 0], carry)
    o = acc * (1.0 / l)
    for g in range(Q_PER_KV):
        o_ref[0, :, g * HEAD_DIM:(g + 1) * HEAD_DIM] = o[g * TQ:(g + 1) * TQ].astype(BF16)


def _attention(q, kt, v, kt_c, v_c):
    b, _, s, _ = q.shape
    n_ctx = v_c.shape[2]
    return pl.pallas_call(
        functools.partial(_attn_kernel, n_lat=s),
        grid=(b, N_KV_HEADS, s // TQ),
        in_specs=[
            pl.BlockSpec((1, Q_PER_KV, TQ, HEAD_DIM), lambda bi, g, i: (bi, g, i, 0)),
            pl.BlockSpec((1, 1, HEAD_DIM, s), lambda bi, g, i: (bi, g, 0, 0)),
            pl.BlockSpec((1, 1, s, HEAD_DIM), lambda bi, g, i: (bi, g, 0, 0)),
            pl.BlockSpec((1, 1, HEAD_DIM, n_ctx), lambda bi, g, i: (bi, g, 0, 0)),
            pl.BlockSpec((1, 1, n_ctx, HEAD_DIM), lambda bi, g, i: (bi, g, 0, 0)),
        ],
        out_specs=pl.BlockSpec((1, TQ, Q_PER_KV * HEAD_DIM), lambda bi, g, i: (bi, i, g)),
        out_shape=jax.ShapeDtypeStruct((b, s, ATTN_WIDTH), BF16),
        compiler_params=_cparams(("parallel", "parallel", "arbitrary")),
        name="attn",
    )(q, kt, v, kt_c, v_c)


def _outproj_kernel(x_ref, attn_ref, u_ref, up_ref, un_ref, cb_ref, wc_ref, wo_ref,
                    gt_ref, sh_ref, sc_ref, g2_ref, wr_ref, br_ref,
                    x1_ref, h2_ref, idx_ref, gate_ref, cnt_ref, *, nt):
    bi = pl.program_id(0)
    i = pl.program_id(1)

    u = u_ref[0].astype(F32)
    row = lax.broadcasted_iota(jnp.int32, u.shape, 0)
    prev_row = jnp.where(i > 0, up_ref[0, BF16_SUBLANES - 1:BF16_SUBLANES, :].astype(F32), 0.0)
    next_row = jnp.where(i < nt - 1, un_ref[0, 0:1, :].astype(F32), 0.0)
    u_prev = jnp.where(row == 0, prev_row, pltpu.roll(u, 1, axis=0))
    u_next = jnp.where(row == TM_OUT - 1, next_row, pltpu.roll(u, TM_OUT - 1, axis=0))
    conv = cb_ref[0].astype(F32) * (u_prev * wc_ref[0:1, :] + u * wc_ref[1:2, :] + u_next * wc_ref[2:3, :])

    mix = (jnp.dot(attn_ref[0], wo_ref[0:ATTN_WIDTH, :], preferred_element_type=F32)
           + jnp.dot(conv.astype(BF16), wo_ref[ATTN_WIDTH:, :], preferred_element_type=F32))
    x1 = x_ref[0] + gt_ref[0] * mix
    x1_ref[0] = x1
    h2 = _modulated_norm(x1, g2_ref[...], sh_ref[0], sc_ref[0])
    h2_ref[0] = h2

    hi = h2.astype(BF16)
    lo = (h2 - hi.astype(F32)).astype(BF16)
    r = (jnp.dot(hi, wr_ref[...], preferred_element_type=F32)
         + jnp.dot(lo, wr_ref[...], preferred_element_type=F32))
    logits = r[:, :ROUTER_PAD] + r[:, ROUTER_PAD:] + br_ref[...]
    lt = logits.T[0:N_EXPERTS, :]

    e_iota = lax.broadcasted_iota(jnp.int32, lt.shape, 0)
    work = lt
    vals, idxs = [], []
    cnt = jnp.zeros((N_EXPERTS, 1), F32)
    for _ in range(TOP_K):
        m = jnp.max(work, axis=0, keepdims=True)
        idx = jnp.min(jnp.where(work == m, e_iota, N_EXPERTS), axis=0, keepdims=True)
        hit = e_iota == idx
        cnt = cnt + jnp.sum(hit.astype(F32), axis=1, keepdims=True)
        work = jnp.where(hit, -jnp.inf, work)
        vals.append(m)
        idxs.append(idx)
    ex = [jnp.exp(v - vals[0]) for v in vals]
    inv = 1.0 / (ex[0] + ex[1] + ex[2] + ex[3])
    idx_ref[...] = jnp.concatenate(idxs, axis=0)
    gates = jnp.concatenate([e * inv for e in ex]
                            + [jnp.zeros((LANES - TOP_K, TM_OUT), F32)], axis=0)
    gate_ref[...] = gates.T

    @pl.when((bi == 0) & (i == 0))
    def _():
        cnt_ref[...] = jnp.zeros_like(cnt_ref)

    cnt_ref[...] += jnp.broadcast_to(cnt, cnt_ref.shape)


def _outproj(x, attn, u, cb, w_conv, w_out, gt1, sh2, sc2, g2, wr2, br):
    b, s, d = x.shape
    nt = s // TM_OUT
    n = b * s
    halo = TM_OUT // BF16_SUBLANES
    n_halo = s // BF16_SUBLANES
    tile = lambda w: pl.BlockSpec((1, TM_OUT, w), lambda bi, i: (bi, i, 0))
    mod_spec = pl.BlockSpec((1, 1, d), lambda bi, i: (bi, 0, 0))
    full = lambda r, c: pl.BlockSpec((r, c), lambda bi, i: (0, 0))
    return pl.pallas_call(
        functools.partial(_outproj_kernel, nt=nt),
        grid=(b, nt),
        in_specs=[
            tile(d), tile(ATTN_WIDTH), tile(CONV_WIDTH),
            pl.BlockSpec((1, BF16_SUBLANES, CONV_WIDTH),
                         lambda bi, i: (bi, jnp.maximum(i * halo - 1, 0), 0)),
            pl.BlockSpec((1, BF16_SUBLANES, CONV_WIDTH),
                         lambda bi, i: (bi, jnp.minimum((i + 1) * halo, n_halo - 1), 0)),
            tile(CONV_WIDTH),
            full(CONV_K, CONV_WIDTH), full(d, d),
            mod_spec, mod_spec, mod_spec, full(1, d),
            full(d, 2 * ROUTER_PAD), full(1, ROUTER_PAD),
        ],
        out_specs=[
            tile(d), tile(d),
            pl.BlockSpec((TOP_K, TM_OUT), lambda bi, i: (0, bi * nt + i)),
            pl.BlockSpec((TM_OUT, LANES), lambda bi, i: (bi * nt + i, 0)),
            pl.BlockSpec((N_EXPERTS, LANES), lambda bi, i: (0, 0)),
        ],
        out_shape=[
            jax.ShapeDtypeStruct((b, s, d), F32),
            jax.ShapeDtypeStruct((b, s, d), F32),
            jax.ShapeDtypeStruct((TOP_K, n), jnp.int32),
            jax.ShapeDtypeStruct((n, LANES), F32),
            jax.ShapeDtypeStruct((N_EXPERTS, LANES), F32),
        ],
        compiler_params=_cparams(("arbitrary", "arbitrary")),
        name="outproj",
    )(x, attn, u, u, u, cb, w_conv, w_out, gt1, sh2, sc2, g2.reshape(1, d), wr2, br)


def _slots_kernel(e_ref, tri_ref, base_ref, slot_ref, run_ref):
    @pl.when(pl.program_id(0) == 0)
    def _():
        run_ref[...] = base_ref[...]

    e = e_ref[...]
    onehot = lax.broadcasted_iota(jnp.int32, (N_EXPERTS, SLOT_CHUNK), 0) == e
    oh = jnp.where(onehot, 1.0, 0.0)
    before = jnp.dot(oh.astype(BF16), tri_ref[...], preferred_element_type=F32)
    start = run_ref[:, 0:1]
    slot_ref[...] = jnp.sum(oh * (before + start), axis=0, keepdims=True).astype(jnp.int32)
    run_ref[...] += jnp.broadcast_to(jnp.sum(oh, axis=1, keepdims=True), run_ref.shape)


def _slots(pair_expert, base):
    p = pair_expert.shape[1]
    tri = jnp.asarray(np.triu(np.ones((SLOT_CHUNK, SLOT_CHUNK), np.float32), 1), BF16)
    return pl.pallas_call(
        _slots_kernel,
        grid=(p // SLOT_CHUNK,),
        in_specs=[
            pl.BlockSpec((1, SLOT_CHUNK), lambda j: (0, j)),
            pl.BlockSpec((SLOT_CHUNK, SLOT_CHUNK), lambda j: (0, 0)),
            pl.BlockSpec((N_EXPERTS, LANES), lambda j: (0, 0)),
        ],
        out_specs=pl.BlockSpec((1, SLOT_CHUNK), lambda j: (0, j)),
        out_shape=jax.ShapeDtypeStruct((1, p), jnp.int32),
        scratch_shapes=[pltpu.VMEM((N_EXPERTS, LANES), F32)],
        compiler_params=_cparams(("arbitrary",)),
        name="slots",
    )(pair_expert, tri, base)


def _moe_kernel(code_ref, bexp_ref, nv_ref, h_hbm, wgu_ref, bgu_ref, wd_ref, bd_ref, y_hbm,
                xbuf, ybuf, wgu_bf, wd_bf, sem_g, sem_s, *, n_tok, nb):
    i = pl.program_id(0)
    cur = i % 2
    nxt = 1 - cur

    def gather(blk, slot):
        def issue(r, carry):
            src = code_ref[blk, r] & (n_tok - 1)
            pltpu.make_async_copy(h_hbm.at[pl.ds(src, 1), :], xbuf.at[slot, pl.ds(r, 1), :],
                                  sem_g.at[slot]).start()
            return carry
        lax.fori_loop(0, MOE_BLK, issue, 0, unroll=8)

    def scatter(blk, slot):
        def issue(r, carry):
            dst = code_ref[blk, r]
            pltpu.make_async_copy(ybuf.at[slot, pl.ds(r, 1), :], y_hbm.at[pl.ds(dst, 1), :],
                                  sem_s.at[slot]).start()
            return carry
        lax.fori_loop(0, nv_ref[blk], issue, 0)

    def wait_gather(slot):
        pltpu.make_async_copy(h_hbm.at[pl.ds(0, MOE_BLK), :], xbuf.at[slot], sem_g.at[slot]).wait()

    def wait_scatter(blk, slot):
        rows = nv_ref[blk]
        whole = pl.multiple_of((rows >> 3) << 3, 8)

        @pl.when(whole > 0)
        def _():
            pltpu.make_async_copy(ybuf.at[slot, pl.ds(0, whole), :], y_hbm.at[pl.ds(0, whole), :],
                                  sem_s.at[slot]).wait()

        for part in (4, 2, 1):
            @pl.when((rows & part) != 0)
            def _():
                pltpu.make_async_copy(ybuf.at[slot, pl.ds(0, part), :], y_hbm.at[pl.ds(0, part), :],
                                      sem_s.at[slot]).wait()

    @pl.when((i == 0) & (nv_ref[0] > 0))
    def _():
        gather(0, 0)

    @pl.when((i + 1 < nb) & (nv_ref[jnp.minimum(i + 1, nb - 1)] > 0))
    def _():
        gather(i + 1, nxt)

    @pl.when(i >= 2)
    def _():
        wait_scatter(i - 2, cur)

    @pl.when(nv_ref[i] > 0)
    def _():
        prev_e = bexp_ref[jnp.maximum(i - 1, 0)]

        @pl.when((i == 0) | (bexp_ref[i] != prev_e))
        def _():
            wgu_bf[...] = wgu_ref[0].astype(BF16)
            wd_bf[...] = wd_ref[0].astype(BF16)

        wait_gather(cur)
        xb = xbuf[cur].astype(BF16)
        gu = jnp.dot(xb, wgu_bf[...], preferred_element_type=F32) + bgu_ref[0]
        glu = jnp.minimum(gu[:, :D_FF], SWIGLU_LIMIT)
        lin = jnp.clip(gu[:, D_FF:], -SWIGLU_LIMIT, SWIGLU_LIMIT)
        act = glu * jax.nn.sigmoid(SWIGLU_ALPHA * glu) * (lin + 1.0)
        ybuf[cur] = jnp.dot(act.astype(BF16), wd_bf[...], preferred_element_type=F32) + bd_ref[0]
        scatter(i, cur)

    @pl.when(i == nb - 1)
    def _():
        wait_scatter(nb - 2, nxt)
        wait_scatter(nb - 1, cur)


def _moe(code, block_expert, n_valid, h2, w_gate_up, b_gate_up, w_down, b_down, n_rows_out):
    n_tok, d = h2.shape
    nb = code.shape[0]
    grid_spec = pltpu.PrefetchScalarGridSpec(
        num_scalar_prefetch=3,
        grid=(nb,),
        in_specs=[
            pl.BlockSpec(memory_space=pl.ANY),
            pl.BlockSpec((1, d, 2 * D_FF), lambda i, code, be, nv: (be[i], 0, 0)),
            pl.BlockSpec((1, 1, 2 * D_FF), lambda i, code, be, nv: (be[i], 0, 0)),
            pl.BlockSpec((1, D_FF, d), lambda i, code, be, nv: (be[i], 0, 0)),
            pl.BlockSpec((1, 1, d), lambda i, code, be, nv: (be[i], 0, 0)),
        ],
        out_specs=pl.BlockSpec(memory_space=pl.ANY),
        scratch_shapes=[
            pltpu.VMEM((2, MOE_BLK, d), F32),
            pltpu.VMEM((2, MOE_BLK, d), F32),
            pltpu.VMEM((d, 2 * D_FF), BF16),
            pltpu.VMEM((D_FF, d), BF16),
            pltpu.SemaphoreType.DMA((2,)),
            pltpu.SemaphoreType.DMA((2,)),
        ],
    )
    return pl.pallas_call(
        functools.partial(_moe_kernel, n_tok=n_tok, nb=nb),
        grid_spec=grid_spec,
        out_shape=jax.ShapeDtypeStruct((n_rows_out, d), F32),
        compiler_params=_cparams(("arbitrary",)),
        name="moe",
    )(code, block_expert, n_valid, h2, w_gate_up, b_gate_up.reshape(N_EXPERTS, 1, 2 * D_FF),
      w_down, b_down.reshape(N_EXPERTS, 1, d))


def _combine_kernel(x1_ref, y0_ref, y1_ref, y2_ref, y3_ref, gate_ref, gt_ref, gf_ref, o_ref):
    g = gate_ref[...]
    moe = (g[:, 0:1] * y0_ref[...] + g[:, 1:2] * y1_ref[...]
           + g[:, 2:3] * y2_ref[...] + g[:, 3:4] * y3_ref[...])
    x2 = x1_ref[...] + gt_ref[0] * moe
    ms = jnp.mean(x2 * x2, axis=-1, keepdims=True)
    o_ref[...] = x2 * lax.rsqrt(ms + NORM_EPS) * gf_ref[...]


def _combine(x1, y, gate_t, gt2, g_final, s):
    n, d = x1.shape
    nt = n // TM_FIN
    per_seq = s // TM_FIN
    y_spec = lambda k: pl.BlockSpec((TM_FIN, d), lambda i, k=k: (k * nt + i, 0))
    return pl.pallas_call(
        _combine_kernel,
        grid=(nt,),
        in_specs=[
            pl.BlockSpec((TM_FIN, d), lambda i: (i, 0)),
            y_spec(0), y_spec(1), y_spec(2), y_spec(3),
            pl.BlockSpec((TM_FIN, LANES), lambda i: (i, 0)),
            pl.BlockSpec((1, 1, d), lambda i: (i // per_seq, 0, 0)),
            pl.BlockSpec((1, d), lambda i: (0, 0)),
        ],
        out_specs=pl.BlockSpec((TM_FIN, d), lambda i: (i, 0)),
        out_shape=jax.ShapeDtypeStruct((n, d), F32),
        compiler_params=_cparams(("parallel",)),
        name="combine",
    )(x1, y, y, y, y, gate_t, gt2, g_final.reshape(1, d))


def kernel(x, c, ctx, c_ctx, w_ada, b_ada, g_norm1, g_norm2, w_in, g_q, g_k, w_conv, w_out,
           w_router, b_router, w_gate_up, b_gate_up, w_down, b_down, g_final):
    b, s, d = x.shape
    n = b * s
    assert w_ada.shape[0] == 1 and d == D_MODEL and b + 1 <= ADA_ROWS
    assert s % TM_IN == 0 and s % TM_OUT == 0 and s % TQ == 0 and s % TK == 0
    assert n & (n - 1) == 0 and (TOP_K * n) % SLOT_CHUNK == 0

    cond = jnp.concatenate([c, c_ctx[None], jnp.zeros((ADA_ROWS - b - 1, d), F32)], axis=0)
    mod = _ada(cond, w_ada[0], b_ada[0])
    sh1, sc1, gt1, sh2, sc2, gt2 = [m[:b, None, :] for m in jnp.split(mod, N_MOD, axis=-1)]
    csh1, csc1 = [m[b:b + 1] for m in jnp.split(mod, N_MOD, axis=-1)[:2]]

    w_in_b = w_in[0].astype(BF16)
    cos_t, sin_t = _rope_tables(s)
    q, kt, v, u, cb = _inproj(x, sh1, sc1, g_norm1[0], w_in_b, g_q[0], g_k[0], cos_t, sin_t)
    kt_c, v_c = _ctx_kv(ctx, csh1, csc1, g_norm1[0], w_in_b, g_k[0])
    attn = _attention(q, kt, v, kt_c, v_c)

    wr = jnp.pad(w_router[0], ((0, 0), (0, ROUTER_PAD - N_EXPERTS)))
    wr_hi = wr.astype(BF16)
    wr_lo = (wr - wr_hi.astype(F32)).astype(BF16)
    wr2 = jnp.concatenate([wr_hi, wr_lo], axis=1)
    br = jnp.pad(b_router[0], (0, ROUTER_PAD - N_EXPERTS)).reshape(1, ROUTER_PAD)
    x1, h2, top_idx, gate_t, counts = _outproj(x, attn, u, cb, w_conv[0], w_out[0].astype(BF16),
                                               gt1, sh2, sc2, g_norm2[0], wr2, br)

    n_pair = TOP_K * n
    nb = -(-(n_pair + N_EXPERTS * (MOE_BLK - 1)) // MOE_BLK)
    n_slots = nb * MOE_BLK
    cnt = counts[:, 0].astype(jnp.int32)
    padded = (cnt + MOE_BLK - 1) // MOE_BLK * MOE_BLK
    pad_end = jnp.cumsum(padded)
    pad_start = pad_end - padded
    base = jnp.broadcast_to(pad_start.astype(F32)[:, None], (N_EXPERTS, LANES))
    slot = _slots(top_idx.reshape(1, n_pair), base)[0]
    block_start = jnp.arange(nb, dtype=jnp.int32) * MOE_BLK
    block_expert = jnp.minimum(jnp.searchsorted(pad_end, block_start, side='right'),
                               N_EXPERTS - 1).astype(jnp.int32)
    n_valid = jnp.clip((pad_start + cnt)[block_expert] - block_start, 0, MOE_BLK).astype(jnp.int32)
    code = jnp.zeros((n_slots,), jnp.int32).at[slot].set(
        jnp.arange(n_pair, dtype=jnp.int32)).reshape(nb, MOE_BLK)

    y = _moe(code, block_expert, n_valid, h2.reshape(n, d), w_gate_up[0], b_gate_up[0],
             w_down[0], b_down[0], n_pair)
    out = _combine(x1.reshape(n, d), y, gate_t, gt2, g_final, s)
    return out.reshape(b, s, d)
```

```python
import functools

import jax
import jax.numpy as jnp
import numpy as np
from jax import lax
from jax.experimental import pallas as pl
from jax.experimental.pallas import tpu as pltpu

F32 = jnp.float32
BF16 = jnp.bfloat16

D_MODEL = 1024
HEAD_DIM = 128
N_Q_HEADS = 4
N_KV_HEADS = 2
Q_PER_KV = N_Q_HEADS // N_KV_HEADS
ATTN_WIDTH = N_Q_HEADS * HEAD_DIM
KV_WIDTH = N_KV_HEADS * HEAD_DIM
AXIS_DIM = HEAD_DIM // 2
ROPE_HALF = AXIS_DIM // 2
ROPE_THETA = 10000.0
GRID_W = 64
CONV_WIDTH = D_MODEL - ATTN_WIDTH
CONV_K = 3
Q_END = ATTN_WIDTH
K_END = Q_END + KV_WIDTH
KV_END = K_END + KV_WIDTH
CB_END = KV_END + CONV_WIDTH
CC_END = CB_END + CONV_WIDTH
IN_COLS = CC_END + CONV_WIDTH
N_EXPERTS = 32
TOP_K = 4
D_FF = D_MODEL
SWIGLU_LIMIT = 7.0
SWIGLU_ALPHA = 1.702
N_MOD = 6
NORM_EPS = 1e-6
LOG2E = 1.4426950408889634

LANES = 128
BF16_SUBLANES = 16
VMEM_LIMIT = 56 * 1024 * 1024

ADA_ROWS = 8
ADA_TN = 1536
TM_IN = 512
TQ = 256
TK = 512
TM_OUT = 512
ROUTER_PAD = LANES
SLOT_CHUNK = 512
MOE_BLK = 256
TM_FIN = 512


def _cparams(sem, vmem=VMEM_LIMIT):
    return pltpu.CompilerParams(dimension_semantics=sem, vmem_limit_bytes=vmem)


def _ada_kernel(c_ref, w_ref, b_ref, o_ref):
    c = c_ref[...]
    a = c * jax.nn.sigmoid(c)
    o_ref[...] = jnp.dot(a, w_ref[...], preferred_element_type=F32,
                         precision=lax.Precision.HIGHEST) + b_ref[...]


def _ada(cond, w_ada, b_ada):
    n = w_ada.shape[1]
    return pl.pallas_call(
        _ada_kernel,
        grid=(n // ADA_TN,),
        in_specs=[
            pl.BlockSpec((ADA_ROWS, D_MODEL), lambda j: (0, 0)),
            pl.BlockSpec((D_MODEL, ADA_TN), lambda j: (0, j)),
            pl.BlockSpec((1, ADA_TN), lambda j: (0, j)),
        ],
        out_specs=pl.BlockSpec((ADA_ROWS, ADA_TN), lambda j: (0, j)),
        out_shape=jax.ShapeDtypeStruct((ADA_ROWS, n), F32),
        compiler_params=_cparams(("arbitrary",)),
        name="ada",
    )(cond, w_ada, b_ada.reshape(1, n))


def _modulated_norm(x, gain, shift, scale):
    ms = jnp.mean(x * x, axis=-1, keepdims=True)
    return x * lax.rsqrt(ms + NORM_EPS) * gain * (1.0 + scale) + shift


def _head_norm(t, gain):
    ms = jnp.mean(t * t, axis=-1, keepdims=True)
    return t * lax.rsqrt(ms + NORM_EPS) * gain


def _rope(t, cos, sin_signed):
    lane = lax.broadcasted_iota(jnp.int32, t.shape, 1)
    first = (lane & (AXIS_DIM - 1)) < ROPE_HALF
    fwd = pltpu.roll(t, HEAD_DIM - ROPE_HALF, axis=1)
    bwd = pltpu.roll(t, ROPE_HALF, axis=1)
    return t * cos + jnp.where(first, fwd, bwd) * sin_signed


def _rope_tables(n_tokens):
    rows = n_tokens // GRID_W
    row = jnp.broadcast_to(jnp.arange(rows, dtype=F32)[:, None], (rows, GRID_W)).reshape(-1)
    col = jnp.broadcast_to(jnp.arange(GRID_W, dtype=F32)[None, :], (rows, GRID_W)).reshape(-1)
    inv_freq = ROPE_THETA ** (-jnp.arange(0, AXIS_DIM, 2, dtype=F32) / AXIS_DIM)
    ang = jnp.stack([row[:, None] * inv_freq, col[:, None] * inv_freq], axis=1)
    cos, sin = jnp.cos(ang), jnp.sin(ang)
    cos_t = jnp.stack([cos, cos], axis=2).reshape(n_tokens, HEAD_DIM)
    sin_t = jnp.stack([-sin, sin], axis=2).reshape(n_tokens, HEAD_DIM)
    return cos_t, sin_t


def _inproj_kernel(x_ref, sh_ref, sc_ref, g1_ref, w_ref, gq_ref, gk_ref, cos_ref, sin_ref,
                   qt_ref, k_ref, vt_ref, u_ref, cb_ref):
    h = _modulated_norm(x_ref[0], g1_ref[...], sh_ref[0], sc_ref[0]).astype(BF16)
    cos = cos_ref[...]
    sin = sin_ref[...]
    q_scale = HEAD_DIM ** -0.5 * LOG2E

    q = jnp.dot(h, w_ref[:, 0:Q_END], preferred_element_type=F32)
    for hh in range(N_Q_HEADS):
        t = _head_norm(q[:, hh * HEAD_DIM:(hh + 1) * HEAD_DIM], gq_ref[...])
        qt_ref[0, hh] = (_rope(t, cos, sin) * q_scale).T.astype(BF16)

    k = jnp.dot(h, w_ref[:, Q_END:K_END], preferred_element_type=F32)
    for g in range(N_KV_HEADS):
        t = _head_norm(k[:, g * HEAD_DIM:(g + 1) * HEAD_DIM], gk_ref[...])
        k_ref[0, g] = _rope(t, cos, sin).astype(BF16)

    v = jnp.dot(h, w_ref[:, K_END:KV_END], preferred_element_type=F32)
    for g in range(N_KV_HEADS):
        vt_ref[0, g] = v[:, g * HEAD_DIM:(g + 1) * HEAD_DIM].T.astype(BF16)

    cb_ref[0] = jnp.dot(h, w_ref[:, KV_END:CB_END], preferred_element_type=F32).astype(BF16)
    cc = jnp.dot(h, w_ref[:, CB_END:CC_END], preferred_element_type=F32)
    cx = jnp.dot(h, w_ref[:, CC_END:IN_COLS], preferred_element_type=F32)
    u_ref[0] = (cc * cx).astype(BF16)


def _inproj(x, sh1, sc1, g1, w_in, g_q, g_k, cos_t, sin_t):
    b, s, d = x.shape
    nt = s // TM_IN
    mod_spec = pl.BlockSpec((1, 1, d), lambda bi, i: (bi, 0, 0))
    vec_spec = lambda n: pl.BlockSpec((1, n), lambda bi, i: (0, 0))
    return pl.pallas_call(
        _inproj_kernel,
        grid=(b, nt),
        in_specs=[
            pl.BlockSpec((1, TM_IN, d), lambda bi, i: (bi, i, 0)),
            mod_spec, mod_spec, vec_spec(d),
            pl.BlockSpec((d, IN_COLS), lambda bi, i: (0, 0)),
            vec_spec(HEAD_DIM), vec_spec(HEAD_DIM),
            pl.BlockSpec((TM_IN, HEAD_DIM), lambda bi, i: (i, 0)),
            pl.BlockSpec((TM_IN, HEAD_DIM), lambda bi, i: (i, 0)),
        ],
        out_specs=[
            pl.BlockSpec((1, N_Q_HEADS, HEAD_DIM, TM_IN), lambda bi, i: (bi, 0, 0, i)),
            pl.BlockSpec((1, N_KV_HEADS, TM_IN, HEAD_DIM), lambda bi, i: (bi, 0, i, 0)),
            pl.BlockSpec((1, N_KV_HEADS, HEAD_DIM, TM_IN), lambda bi, i: (bi, 0, 0, i)),
            pl.BlockSpec((1, TM_IN, CONV_WIDTH), lambda bi, i: (bi, i, 0)),
            pl.BlockSpec((1, TM_IN, CONV_WIDTH), lambda bi, i: (bi, i, 0)),
        ],
        out_shape=[
            jax.ShapeDtypeStruct((b, N_Q_HEADS, HEAD_DIM, s), BF16),
            jax.ShapeDtypeStruct((b, N_KV_HEADS, s, HEAD_DIM), BF16),
            jax.ShapeDtypeStruct((b, N_KV_HEADS, HEAD_DIM, s), BF16),
            jax.ShapeDtypeStruct((b, s, CONV_WIDTH), BF16),
            jax.ShapeDtypeStruct((b, s, CONV_WIDTH), BF16),
        ],
        compiler_params=_cparams(("parallel", "arbitrary")),
        name="inproj",
    )(x, sh1, sc1, g1.reshape(1, d), w_in, g_q.reshape(1, HEAD_DIM), g_k.reshape(1, HEAD_DIM),
      cos_t, sin_t)


def _ctx_kv_kernel(x_ref, sh_ref, sc_ref, g1_ref, w_ref, gk_ref, k_ref, vt_ref):
    h = _modulated_norm(x_ref[0], g1_ref[...], sh_ref[...], sc_ref[...]).astype(BF16)
    k = jnp.dot(h, w_ref[:, Q_END:K_END], preferred_element_type=F32)
    for g in range(N_KV_HEADS):
        k_ref[0, g] = _head_norm(k[:, g * HEAD_DIM:(g + 1) * HEAD_DIM], gk_ref[...]).astype(BF16)
    v = jnp.dot(h, w_ref[:, K_END:KV_END], preferred_element_type=F32)
    for g in range(N_KV_HEADS):
        vt_ref[0, g] = v[:, g * HEAD_DIM:(g + 1) * HEAD_DIM].T.astype(BF16)


def _ctx_kv(ctx, csh1, csc1, g1, w_in, g_k):
    b, n_ctx, d = ctx.shape
    vec_spec = lambda n: pl.BlockSpec((1, n), lambda bi: (0, 0))
    return pl.pallas_call(
        _ctx_kv_kernel,
        grid=(b,),
        in_specs=[
            pl.BlockSpec((1, n_ctx, d), lambda bi: (bi, 0, 0)),
            vec_spec(d), vec_spec(d), vec_spec(d),
            pl.BlockSpec((d, IN_COLS), lambda bi: (0, 0)),
            vec_spec(HEAD_DIM),
        ],
        out_specs=[
            pl.BlockSpec((1, N_KV_HEADS, n_ctx, HEAD_DIM), lambda bi: (bi, 0, 0, 0)),
            pl.BlockSpec((1, N_KV_HEADS, HEAD_DIM, n_ctx), lambda bi: (bi, 0, 0, 0)),
        ],
        out_shape=[
            jax.ShapeDtypeStruct((b, N_KV_HEADS, n_ctx, HEAD_DIM), BF16),
            jax.ShapeDtypeStruct((b, N_KV_HEADS, HEAD_DIM, n_ctx), BF16),
        ],
        compiler_params=_cparams(("arbitrary",)),
        name="ctx_kv",
    )(ctx, csh1, csc1, g1.reshape(1, d), w_in, g_k.reshape(1, HEAD_DIM))


def _attn_kernel(qt_ref, k_ref, vt_ref, kc_ref, vtc_ref, o_ref, s_buf, p_buf, *, n_lat):
    n_chunks = n_lat // TK
    qt = jnp.concatenate([qt_ref[0, g] for g in range(Q_PER_KV)], axis=1)

    def scores(j):
        off = pl.multiple_of(j * TK, TK)
        return jnp.dot(k_ref[0, 0, pl.ds(off, TK), :], qt, preferred_element_type=F32)

    def pv(j, slot):
        off = pl.multiple_of(j * TK, TK)
        return jnp.dot(vt_ref[0, 0, :, pl.ds(off, TK)], p_buf[slot], preferred_element_type=F32)

    def softmax_update(slot, m, l):
        s = s_buf[slot]
        m_new = jnp.maximum(m, jnp.max(s, axis=0, keepdims=True))
        alpha = jnp.exp2(m - m_new)
        p = jnp.exp2(s - m_new)
        p_buf[slot] = p.astype(BF16)
        return m_new, alpha, alpha * l + jnp.sum(p, axis=0, keepdims=True)

    s_c = jnp.dot(kc_ref[0, 0], qt, preferred_element_type=F32)
    m = jnp.max(s_c, axis=0, keepdims=True)
    p_c = jnp.exp2(s_c - m)
    l = jnp.sum(p_c, axis=0, keepdims=True)
    acc = jnp.dot(vtc_ref[0, 0], p_c.astype(BF16), preferred_element_type=F32)

    s_buf[0] = scores(0)
    s_buf[1] = scores(1)
    m, alpha, l = softmax_update(0, m, l)

    def half(j, slot, carry, has_next=True):
        m, alpha_prev, l, acc = carry
        other = 1 - slot
        acc = alpha_prev * acc + pv(j - 1, other)
        m, alpha, l = softmax_update(slot, m, l)
        if has_next:
            s_buf[other] = scores(j + 1)
        return m, alpha, l, acc

    def body(jj, carry):
        j = 2 * jj + 1
        return half(j + 1, 0, half(j, 1, carry))

    carry = lax.fori_loop(0, (n_chunks - 2) // 2, body, (m, alpha, l, acc))
    m, alpha, l, acc = half(n_chunks - 1, 1, carry, has_next=False)
    acc = alpha * acc + pv(n_chunks - 1, 1)
    o_t = acc * (1.0 / l)
    for g in range(Q_PER_KV):
        o_ref[0, :, g * HEAD_DIM:(g + 1) * HEAD_DIM] = o_t[:, g * TQ:(g + 1) * TQ].T.astype(BF16)


def _attention(qt, k, vt, k_c, vt_c):
    b, _, _, s = qt.shape
    n_ctx = k_c.shape[2]
    rows = Q_PER_KV * TQ
    return pl.pallas_call(
        functools.partial(_attn_kernel, n_lat=s),
        grid=(b, N_KV_HEADS, s // TQ),
        in_specs=[
            pl.BlockSpec((1, Q_PER_KV, HEAD_DIM, TQ), lambda bi, g, i: (bi, g, 0, i)),
            pl.BlockSpec((1, 1, s, HEAD_DIM), lambda bi, g, i: (bi, g, 0, 0)),
            pl.BlockSpec((1, 1, HEAD_DIM, s), lambda bi, g, i: (bi, g, 0, 0)),
            pl.BlockSpec((1, 1, n_ctx, HEAD_DIM), lambda bi, g, i: (bi, g, 0, 0)),
            pl.BlockSpec((1, 1, HEAD_DIM, n_ctx), lambda bi, g, i: (bi, g, 0, 0)),
        ],
        out_specs=pl.BlockSpec((1, TQ, Q_PER_KV * HEAD_DIM), lambda bi, g, i: (bi, i, g)),
        out_shape=jax.ShapeDtypeStruct((b, s, ATTN_WIDTH), BF16),
        scratch_shapes=[
            pltpu.VMEM((2, TK, rows), F32),
            pltpu.VMEM((2, TK, rows), BF16),
        ],
        compiler_params=_cparams(("parallel", "parallel", "arbitrary")),
        name="attn",
    )(qt, k, vt, k_c, vt_c)


def _outproj_kernel(x_ref, attn_ref, u_ref, up_ref, un_ref, cb_ref, wc_ref, wo_ref,
                    gt_ref, sh_ref, sc_ref, g2_ref, wr_ref, br_ref,
                    x1_ref, h2_ref, idx_ref, gate_ref, cnt_ref, *, nt):
    bi = pl.program_id(0)
    i = pl.program_id(1)

    u = u_ref[0].astype(F32)
    row = lax.broadcasted_iota(jnp.int32, u.shape, 0)
    prev_row = jnp.where(i > 0, up_ref[0, BF16_SUBLANES - 1:BF16_SUBLANES, :].astype(F32), 0.0)
    next_row = jnp.where(i < nt - 1, un_ref[0, 0:1, :].astype(F32), 0.0)
    u_prev = jnp.where(row == 0, prev_row, pltpu.roll(u, 1, axis=0))
    u_next = jnp.where(row == TM_OUT - 1, next_row, pltpu.roll(u, TM_OUT - 1, axis=0))
    conv = cb_ref[0].astype(F32) * (u_prev * wc_ref[0:1, :] + u * wc_ref[1:2, :] + u_next * wc_ref[2:3, :])

    mix = (jnp.dot(attn_ref[0], wo_ref[0:ATTN_WIDTH, :], preferred_element_type=F32)
           + jnp.dot(conv.astype(BF16), wo_ref[ATTN_WIDTH:, :], preferred_element_type=F32))
    x1 = x_ref[0] + gt_ref[0] * mix
    x1_ref[0] = x1
    h2 = _modulated_norm(x1, g2_ref[...], sh_ref[0], sc_ref[0])
    h2_ref[0] = h2

    hi = h2.astype(BF16)
    lo = (h2 - hi.astype(F32)).astype(BF16)
    r = (jnp.dot(hi, wr_ref[...], preferred_element_type=F32)
         + jnp.dot(lo, wr_ref[...], preferred_element_type=F32))
    logits = r[:, :ROUTER_PAD] + r[:, ROUTER_PAD:] + br_ref[...]
    lt = logits.T[0:N_EXPERTS, :]

    e_iota = lax.broadcasted_iota(jnp.int32, lt.shape, 0)
    work = lt
    vals, idxs = [], []
    cnt = jnp.zeros((N_EXPERTS, 1), F32)
    for _ in range(TOP_K):
        m = jnp.max(work, axis=0, keepdims=True)
        idx = jnp.min(jnp.where(work == m, e_iota, N_EXPERTS), axis=0, keepdims=True)
        hit = e_iota == idx
        cnt = cnt + jnp.sum(hit.astype(F32), axis=1, keepdims=True)
        work = jnp.where(hit, -jnp.inf, work)
        vals.append(m)
        idxs.append(idx)
    ex = [jnp.exp(v - vals[0]) for v in vals]
    inv = 1.0 / (ex[0] + ex[1] + ex[2] + ex[3])
    idx_ref[...] = jnp.concatenate(idxs, axis=0)
    gates = jnp.concatenate([e * inv for e in ex]
                            + [jnp.zeros((LANES - TOP_K, TM_OUT), F32)], axis=0)
    gate_ref[...] = gates.T

    @pl.when((bi == 0) & (i == 0))
    def _():
        cnt_ref[...] = jnp.zeros_like(cnt_ref)

    cnt_ref[...] += jnp.broadcast_to(cnt, cnt_ref.shape)


def _outproj(x, attn, u, cb, w_conv, w_out, gt1, sh2, sc2, g2, wr2, br):
    b, s, d = x.shape
    nt = s // TM_OUT
    n = b * s
    halo = TM_OUT // BF16_SUBLANES
    n_halo = s // BF16_SUBLANES
    tile = lambda w: pl.BlockSpec((1, TM_OUT, w), lambda bi, i: (bi, i, 0))
    mod_spec = pl.BlockSpec((1, 1, d), lambda bi, i: (bi, 0, 0))
    full = lambda r, c: pl.BlockSpec((r, c), lambda bi, i: (0, 0))
    return pl.pallas_call(
        functools.partial(_outproj_kernel, nt=nt),
        grid=(b, nt),
        in_specs=[
            tile(d), tile(ATTN_WIDTH), tile(CONV_WIDTH),
            pl.BlockSpec((1, BF16_SUBLANES, CONV_WIDTH),
                         lambda bi, i: (bi, jnp.maximum(i * halo - 1, 0), 0)),
            pl.BlockSpec((1, BF16_SUBLANES, CONV_WIDTH),
                         lambda bi, i: (bi, jnp.minimum((i + 1) * halo, n_halo - 1), 0)),
            tile(CONV_WIDTH),
            full(CONV_K, CONV_WIDTH), full(d, d),
            mod_spec, mod_spec, mod_spec, full(1, d),
            full(d, 2 * ROUTER_PAD), full(1, ROUTER_PAD),
        ],
        out_specs=[
            tile(d), tile(d),
            pl.BlockSpec((TOP_K, TM_OUT), lambda bi, i: (0, bi * nt + i)),
            pl.BlockSpec((TM_OUT, LANES), lambda bi, i: (bi * nt + i, 0)),
            pl.BlockSpec((N_EXPERTS, LANES), lambda bi, i: (0, 0)),
        ],
        out_shape=[
            jax.ShapeDtypeStruct((b, s, d), F32),
            jax.ShapeDtypeStruct((b, s, d), F32),
            jax.ShapeDtypeStruct((TOP_K, n), jnp.int32),
            jax.ShapeDtypeStruct((n, LANES), F32),
            jax.ShapeDtypeStruct((N_EXPERTS, LANES), F32),
        ],
        compiler_params=_cparams(("arbitrary", "arbitrary")),
        name="outproj",
    )(x, attn, u, u, u, cb, w_conv, w_out, gt1, sh2, sc2, g2.reshape(1, d), wr2, br)


def _slots_kernel(e_ref, tri_ref, base_ref, slot_ref, run_ref):
    @pl.when(pl.program_id(0) == 0)
    def _():
        run_ref[...] = base_ref[...]

    e = e_ref[...]
    onehot = lax.broadcasted_iota(jnp.int32, (N_EXPERTS, SLOT_CHUNK), 0) == e
    oh = jnp.where(onehot, 1.0, 0.0)
    before = jnp.dot(oh.astype(BF16), tri_ref[...], preferred_element_type=F32)
    start = run_ref[:, 0:1]
    slot_ref[...] = jnp.sum(oh * (before + start), axis=0, keepdims=True).astype(jnp.int32)
    run_ref[...] += jnp.broadcast_to(jnp.sum(oh, axis=1, keepdims=True), run_ref.shape)


def _slots(pair_expert, base):
    p = pair_expert.shape[1]
    tri = jnp.asarray(np.triu(np.ones((SLOT_CHUNK, SLOT_CHUNK), np.float32), 1), BF16)
    return pl.pallas_call(
        _slots_kernel,
        grid=(p // SLOT_CHUNK,),
        in_specs=[
            pl.BlockSpec((1, SLOT_CHUNK), lambda j: (0, j)),
            pl.BlockSpec((SLOT_CHUNK, SLOT_CHUNK), lambda j: (0, 0)),
            pl.BlockSpec((N_EXPERTS, LANES), lambda j: (0, 0)),
        ],
        out_specs=pl.BlockSpec((1, SLOT_CHUNK), lambda j: (0, j)),
        out_shape=jax.ShapeDtypeStruct((1, p), jnp.int32),
        scratch_shapes=[pltpu.VMEM((N_EXPERTS, LANES), F32)],
        compiler_params=_cparams(("arbitrary",)),
        name="slots",
    )(pair_expert, tri, base)


def _moe_kernel(code_ref, bexp_ref, nv_ref, h_hbm, wgu_ref, bgu_ref, wd_ref, bd_ref, y_hbm,
                xbuf, ybuf, wgu_bf, wd_bf, sem_g, sem_s, *, n_tok, nb):
    i = pl.program_id(0)
    cur = i % 2
    nxt = 1 - cur

    def gather(blk, slot):
        def issue(r, carry):
            src = code_ref[blk, r] & (n_tok - 1)
            pltpu.make_async_copy(h_hbm.at[pl.ds(src, 1), :], xbuf.at[slot, pl.ds(r, 1), :],
                                  sem_g.at[slot]).start()
            return carry
        lax.fori_loop(0, MOE_BLK, issue, 0, unroll=8)

    def scatter(blk, slot):
        def issue(r, carry):
            dst = code_ref[blk, r]
            pltpu.make_async_copy(ybuf.at[slot, pl.ds(r, 1), :], y_hbm.at[pl.ds(dst, 1), :],
                                  sem_s.at[slot]).start()
            return carry
        lax.fori_loop(0, nv_ref[blk], issue, 0)

    def wait_gather(slot):
        pltpu.make_async_copy(h_hbm.at[pl.ds(0, MOE_BLK), :], xbuf.at[slot], sem_g.at[slot]).wait()

    def wait_scatter(blk, slot):
        rows = nv_ref[blk]
        whole = pl.multiple_of((rows >> 3) << 3, 8)

        @pl.when(whole > 0)
        def _():
            pltpu.make_async_copy(ybuf.at[slot, pl.ds(0, whole), :], y_hbm.at[pl.ds(0, whole), :],
                                  sem_s.at[slot]).wait()

        for part in (4, 2, 1):
            @pl.when((rows & part) != 0)
            def _():
                pltpu.make_async_copy(ybuf.at[slot, pl.ds(0, part), :], y_hbm.at[pl.ds(0, part), :],
                                      sem_s.at[slot]).wait()

    @pl.when((i == 0) & (nv_ref[0] > 0))
    def _():
        gather(0, 0)

    @pl.when((i + 1 < nb) & (nv_ref[jnp.minimum(i + 1, nb - 1)] > 0))
    def _():
        gather(i + 1, nxt)

    @pl.when(i >= 2)
    def _():
        wait_scatter(i - 2, cur)

    @pl.when(nv_ref[i] > 0)
    def _():
        prev_e = bexp_ref[jnp.maximum(i - 1, 0)]

        @pl.when((i == 0) | (bexp_ref[i] != prev_e))
        def _():
            wgu_bf[...] = wgu_ref[0].astype(BF16)
            wd_bf[...] = wd_ref[0].astype(BF16)

        wait_gather(cur)
        xb = xbuf[cur].astype(BF16)
        gu = jnp.dot(xb, wgu_bf[...], preferred_element_type=F32) + bgu_ref[0]
        glu = jnp.minimum(gu[:, :D_FF], SWIGLU_LIMIT)
        lin = jnp.clip(gu[:, D_FF:], -SWIGLU_LIMIT, SWIGLU_LIMIT)
        act = glu * jax.nn.sigmoid(SWIGLU_ALPHA * glu) * (lin + 1.0)
        ybuf[cur] = jnp.dot(act.astype(BF16), wd_bf[...], preferred_element_type=F32) + bd_ref[0]
        scatter(i, cur)

    @pl.when(i == nb - 1)
    def _():
        wait_scatter(nb - 2, nxt)
        wait_scatter(nb - 1, cur)


def _moe(code, block_expert, n_valid, h2, w_gate_up, b_gate_up, w_down, b_down, n_rows_out):
    n_tok, d = h2.shape
    nb = code.shape[0]
    grid_spec = pltpu.PrefetchScalarGridSpec(
        num_scalar_prefetch=3,
        grid=(nb,),
        in_specs=[
            pl.BlockSpec(memory_space=pl.ANY),
            pl.BlockSpec((1, d, 2 * D_FF), lambda i, code, be, nv: (be[i], 0, 0)),
            pl.BlockSpec((1, 1, 2 * D_FF), lambda i, code, be, nv: (be[i], 0, 0)),
            pl.BlockSpec((1, D_FF, d), lambda i, code, be, nv: (be[i], 0, 0)),
            pl.BlockSpec((1, 1, d), lambda i, code, be, nv: (be[i], 0, 0)),
        ],
        out_specs=pl.BlockSpec(memory_space=pl.ANY),
        scratch_shapes=[
            pltpu.VMEM((2, MOE_BLK, d), F32),
            pltpu.VMEM((2, MOE_BLK, d), F32),
            pltpu.VMEM((d, 2 * D_FF), BF16),
            pltpu.VMEM((D_FF, d), BF16),
            pltpu.SemaphoreType.DMA((2,)),
            pltpu.SemaphoreType.DMA((2,)),
        ],
    )
    return pl.pallas_call(
        functools.partial(_moe_kernel, n_tok=n_tok, nb=nb),
        grid_spec=grid_spec,
        out_shape=jax.ShapeDtypeStruct((n_rows_out, d), F32),
        compiler_params=_cparams(("arbitrary",)),
        name="moe",
    )(code, block_expert, n_valid, h2, w_gate_up, b_gate_up.reshape(N_EXPERTS, 1, 2 * D_FF),
      w_down, b_down.reshape(N_EXPERTS, 1, d))


def _combine_kernel(x1_ref, y0_ref, y1_ref, y2_ref, y3_ref, gate_ref, gt_ref, gf_ref, o_ref):
    g = gate_ref[...]
    moe = (g[:, 0:1] * y0_ref[...] + g[:, 1:2] * y1_ref[...]
           + g[:, 2:3] * y2_ref[...] + g[:, 3:4] * y3_ref[...])
    x2 = x1_ref[...] + gt_ref[0] * moe
    ms = jnp.mean(x2 * x2, axis=-1, keepdims=True)
    o_ref[...] = x2 * lax.rsqrt(ms + NORM_EPS) * gf_ref[...]


def _combine(x1, y, gate_t, gt2, g_final, s):
    n, d = x1.shape
    nt = n // TM_FIN
    per_seq = s // TM_FIN
    y_spec = lambda k: pl.BlockSpec((TM_FIN, d), lambda i, k=k: (k * nt + i, 0))
    return pl.pallas_call(
        _combine_kernel,
        grid=(nt,),
        in_specs=[
            pl.BlockSpec((TM_FIN, d), lambda i: (i, 0)),
            y_spec(0), y_spec(1), y_spec(2), y_spec(3),
            pl.BlockSpec((TM_FIN, LANES), lambda i: (i, 0)),
            pl.BlockSpec((1, 1, d), lambda i: (i // per_seq, 0, 0)),
            pl.BlockSpec((1, d), lambda i: (0, 0)),
        ],
        out_specs=pl.BlockSpec((TM_FIN, d), lambda i: (i, 0)),
        out_shape=jax.ShapeDtypeStruct((n, d), F32),
        compiler_params=_cparams(("parallel",)),
        name="combine",
    )(x1, y, y, y, y, gate_t, gt2, g_final.reshape(1, d))


def kernel(x, c, ctx, c_ctx, w_ada, b_ada, g_norm1, g_norm2, w_in, g_q, g_k, w_conv, w_out,
           w_router, b_router, w_gate_up, b_gate_up, w_down, b_down, g_final):
    b, s, d = x.shape
    n = b * s
    assert w_ada.shape[0] == 1 and d == D_MODEL and b + 1 <= ADA_ROWS
    assert s % TM_IN == 0 and s % TM_OUT == 0 and s % TQ == 0 and s % TK == 0
    assert n & (n - 1) == 0 and (TOP_K * n) % SLOT_CHUNK == 0

    cond = jnp.concatenate([c, c_ctx[None], jnp.zeros((ADA_ROWS - b - 1, d), F32)], axis=0)
    mod = _ada(cond, w_ada[0], b_ada[0])
    sh1, sc1, gt1, sh2, sc2, gt2 = [m[:b, None, :] for m in jnp.split(mod, N_MOD, axis=-1)]
    csh1, csc1 = [m[b:b + 1] for m in jnp.split(mod, N_MOD, axis=-1)[:2]]

    w_in_b = w_in[0].astype(BF16)
    cos_t, sin_t = _rope_tables(s)
    qt, k, vt, u, cb = _inproj(x, sh1, sc1, g_norm1[0], w_in_b, g_q[0], g_k[0], cos_t, sin_t)
    k_c, vt_c = _ctx_kv(ctx, csh1, csc1, g_norm1[0], w_in_b, g_k[0])
    attn = _attention(qt, k, vt, k_c, vt_c)

    wr = jnp.pad(w_router[0], ((0, 0), (0, ROUTER_PAD - N_EXPERTS)))
    wr_hi = wr.astype(BF16)
    wr_lo = (wr - wr_hi.astype(F32)).astype(BF16)
    wr2 = jnp.concatenate([wr_hi, wr_lo], axis=1)
    br = jnp.pad(b_router[0], (0, ROUTER_PAD - N_EXPERTS)).reshape(1, ROUTER_PAD)
    x1, h2, top_idx, gate_t, counts = _outproj(x, attn, u, cb, w_conv[0], w_out[0].astype(BF16),
                                               gt1, sh2, sc2, g_norm2[0], wr2, br)

    n_pair = TOP_K * n
    nb = -(-(n_pair + N_EXPERTS * (MOE_BLK - 1)) // MOE_BLK)
    n_slots = nb * MOE_BLK
    cnt = counts[:, 0].astype(jnp.int32)
    padded = (cnt + MOE_BLK - 1) // MOE_BLK * MOE_BLK
    pad_end = jnp.cumsum(padded)
    pad_start = pad_end - padded
    base = jnp.broadcast_to(pad_start.astype(F32)[:, None], (N_EXPERTS, LANES))
    slot = _slots(top_idx.reshape(1, n_pair), base)[0]
    block_start = jnp.arange(nb, dtype=jnp.int32) * MOE_BLK
    block_expert = jnp.minimum(jnp.sum(pad_end[None, :] <= block_start[:, None], axis=1),
                               N_EXPERTS - 1).astype(jnp.int32)
    n_valid = jnp.clip((pad_start + cnt)[block_expert] - block_start, 0, MOE_BLK).astype(jnp.int32)
    code = jnp.zeros((n_slots,), jnp.int32).at[slot].set(
        jnp.arange(n_pair, dtype=jnp.int32)).reshape(nb, MOE_BLK)

    y = _moe(code, block_expert, n_valid, h2.reshape(n, d), w_gate_up[0], b_gate_up[0],
             w_down[0], b_down[0], n_pair)
    out = _combine(x1.reshape(n, d), y, gate_t, gt2, g_final, s)
    return out.reshape(b, s, d)
```
